```python
import math
import jax, jax.numpy as jnp
from jax import lax
import numpy as np

D_MODEL = 1024
BATCH = 8
SEQ = 8192
DEPTH = 1

HEAD_DIM = 64
N_HEADS_A = D_MODEL // (2 * HEAD_DIM)
N_KV_HEADS_A = N_HEADS_A // 4
N_HEADS_B = D_MODEL // (2 * HEAD_DIM)
WIDTH_A = N_HEADS_A * HEAD_DIM
WIDTH_B = N_HEADS_B * HEAD_DIM
MIX_WIDTH = WIDTH_A + WIDTH_B
D_IN_PROJ = (N_HEADS_A + 2 * N_KV_HEADS_A + 3 * N_HEADS_B) * HEAD_DIM
WINDOW_A = 128
DILATED_BRANCHES = ((128, 1), (512, 4), (2048, 16))
BLOCK = 128
ROPE_THETA = 150000.0
REL_BUCKETS = 32
REL_MAX_DISTANCE = 2048
D_FF = -(-8 * D_MODEL // (3 * 256)) * 256
NORM_EPS = 1e-5

kernel_name = "hymba_swa_sink_dilated_hybrid"


def rmsnorm(x, g):
    xf = x.astype(jnp.float32)
    y = xf * lax.rsqrt(jnp.mean(xf * xf, axis=-1, keepdims=True) + NORM_EPS)
    return (y * g.astype(jnp.float32)).astype(x.dtype)


def rope(t, seq_len):
    half = t.shape[-1] // 2
    inv_freq = ROPE_THETA ** (-jnp.arange(half, dtype=jnp.float32) / half)
    ang = jnp.arange(seq_len, dtype=jnp.float32)[:, None] * inv_freq[None, :]
    cos = jnp.cos(ang)[None, :, None, :].astype(t.dtype)
    sin = jnp.sin(ang)[None, :, None, :].astype(t.dtype)
    t1, t2 = t[..., :half], t[..., half:]
    return jnp.concatenate([t1 * cos - t2 * sin, t1 * sin + t2 * cos], axis=-1)


def t5_bucket(dist):
    max_exact = REL_BUCKETS // 2
    df = jnp.maximum(dist, 1).astype(jnp.float32)
    large = max_exact + (jnp.log(df / max_exact) / math.log(REL_MAX_DISTANCE / max_exact)
                         * (REL_BUCKETS - max_exact)).astype(jnp.int32)
    large = jnp.minimum(large, REL_BUCKETS - 1)
    return jnp.where(dist < max_exact, dist, large)


def banded_attention(q, k, v, n_back, bias=None, sinks=None):
    N, H, L, D = q.shape
    Hkv = k.shape[1]
    G = H // Hkv
    nb = L // BLOCK
    qb = q.reshape(N, Hkv, G, nb, BLOCK, D) * (D ** -0.5)

    def windows(t):
        tb = t.reshape(N, Hkv, nb, BLOCK, D)
        prev = jnp.pad(tb, ((0, 0), (0, 0), (1, 0), (0, 0), (0, 0)))[:, :, :-1]
        return jnp.concatenate([prev, tb], axis=3)

    kw, vw = windows(k), windows(v)
    s = jnp.einsum('nkgbqd,nkbsd->nkgbqs', qb, kw).astype(jnp.float32)
    delta = BLOCK + jnp.arange(BLOCK)[:, None] - jnp.arange(2 * BLOCK)[None, :]
    in_band = (delta >= 0) & (delta <= n_back)
    key_exists = (jnp.arange(nb)[:, None] > 0) | (jnp.arange(2 * BLOCK)[None, :] >= BLOCK)
    mask = in_band[None] & key_exists[:, None, :]
    if bias is not None:
        b = bias.astype(jnp.float32)[:, jnp.clip(delta, 0, n_back)]
        s = s + b.reshape(1, Hkv, G, 1, BLOCK, 2 * BLOCK)
    s = jnp.where(mask, s, -jnp.inf)
    m = jnp.max(s, axis=-1)
    if sinks is not None:
        sk = sinks.astype(jnp.float32).reshape(1, Hkv, G, 1, 1)
        m = jnp.maximum(m, sk)
    p = jnp.exp(s - m[..., None])
    denom = jnp.sum(p, axis=-1)
    if sinks is not None:
        denom = denom + jnp.exp(sk - m)
    o = jnp.einsum('nkgbqs,nkbsd->nkgbqd', p, vw.astype(jnp.float32)) / denom[..., None]
    lse = m + jnp.log(denom)
    return o.reshape(N, H, L, D).astype(q.dtype), lse.reshape(N, H, L)


def sliding_window_sink_gqa(q, k, v, sinks):
    S = q.shape[1]
    q, k = rope(q, S), rope(k, S)
    o, _ = banded_attention(q.transpose(0, 2, 1, 3), k.transpose(0, 2, 1, 3),
                            v.transpose(0, 2, 1, 3), WINDOW_A - 1, sinks=sinks)
    return o.transpose(0, 2, 1, 3)


def dilated_mixture(q, k, v, rel_table):
    B, S, H, D = q.shape
    outs, lses = [], []
    for window, dil in DILATED_BRANCHES:
        n_back = window // dil
        span = dil * BLOCK
        s_pad = -(-S // span) * span
        L = s_pad // dil

        def to_sub(t):
            t = jnp.pad(t, ((0, 0), (0, s_pad - S), (0, 0), (0, 0)))
            return t.reshape(B, L, dil, H, D).transpose(0, 2, 3, 1, 4).reshape(B * dil, H, L, D)

        bias = rel_table[t5_bucket(jnp.arange(n_back + 1) * dil)].T
        o, lse = banded_attention(to_sub(q), to_sub(k), to_sub(v), n_back, bias=bias)
        o = o.reshape(B, dil, H, L, D).transpose(0, 3, 1, 2, 4).reshape(B, s_pad, H, D)[:, :S]
        lse = lse.reshape(B, dil, H, L).transpose(0, 3, 1, 2).reshape(B, s_pad, H)[:, :S]
        outs.append(o)
        lses.append(lse)
    w = jax.nn.softmax(jnp.stack(lses, axis=0), axis=0)
    out = jnp.sum(w[..., None] * jnp.stack(outs, axis=0).astype(jnp.float32), axis=0)
    return out.astype(q.dtype)


def setup_inputs(seed: int = 0) -> dict:
    key = jax.random.key(seed)
    ks = jax.random.split(key, 16)
    f32 = jnp.float32
    nrm = lambda k, shape, scale: jax.random.normal(k, shape, f32) * scale
    return {
        "x": jax.random.normal(ks[0], (BATCH, SEQ, D_MODEL), f32),
        "g_attn": 1.0 + nrm(ks[1], (DEPTH, D_MODEL), 0.01),
        "w_in": nrm(ks[2], (DEPTH, D_MODEL, D_IN_PROJ), D_MODEL ** -0.5),
        "b_in": nrm(ks[3], (DEPTH, D_IN_PROJ), 0.01),
        "sinks": nrm(ks[4], (DEPTH, N_HEADS_A), 0.5),
        "rel_table": nrm(ks[5], (REL_BUCKETS, N_HEADS_B), 0.5),
        "g_out_a": 1.0 + nrm(ks[6], (DEPTH, WIDTH_A), 0.01),
        "g_out_b": 1.0 + nrm(ks[7], (DEPTH, WIDTH_B), 0.01),
        "w_o": nrm(ks[8], (DEPTH, MIX_WIDTH, D_MODEL), MIX_WIDTH ** -0.5),
        "g_ffn": 1.0 + nrm(ks[9], (DEPTH, D_MODEL), 0.01),
        "w_gate": nrm(ks[10], (DEPTH, D_MODEL, D_FF), D_MODEL ** -0.5),
        "w_up": nrm(ks[11], (DEPTH, D_MODEL, D_FF), D_MODEL ** -0.5),
        "w_down": nrm(ks[12], (DEPTH, D_FF, D_MODEL), D_FF ** -0.5),
        "g_final": 1.0 + nrm(ks[13], (D_MODEL,), 0.01),
    }


def reference(x, g_attn, w_in, b_in, sinks, rel_table, g_out_a, g_out_b, w_o,
              g_ffn, w_gate, w_up, w_down, g_final):
    B, S, _ = x.shape
    splits = np.cumsum([WIDTH_A, N_KV_HEADS_A * HEAD_DIM, N_KV_HEADS_A * HEAD_DIM,
                        WIDTH_B, WIDTH_B])
    for l in range(DEPTH):
        h = rmsnorm(x, g_attn[l])
        proj = jnp.einsum('bsd,de->bse', h, w_in[l]) + b_in[l]
        qa, ka, va, qb, kb, vb = jnp.split(proj, splits, axis=-1)
        qa = qa.reshape(B, S, N_HEADS_A, HEAD_DIM)
        ka = ka.reshape(B, S, N_KV_HEADS_A, HEAD_DIM)
        va = va.reshape(B, S, N_KV_HEADS_A, HEAD_DIM)
        qb = qb.reshape(B, S, N_HEADS_B, HEAD_DIM)
        kb = kb.reshape(B, S, N_HEADS_B, HEAD_DIM)
        vb = vb.reshape(B, S, N_HEADS_B, HEAD_DIM)
        oa = sliding_window_sink_gqa(qa, ka, va, sinks[l]).reshape(B, S, WIDTH_A)
        ob = dilated_mixture(qb, kb, vb, rel_table).reshape(B, S, WIDTH_B)
        mixed = jnp.concatenate([rmsnorm(oa, g_out_a[l]), rmsnorm(ob, g_out_b[l])], axis=-1)
        x = x + jnp.einsum('bse,ed->bsd', mixed, w_o[l])
        h = rmsnorm(x, g_ffn[l])
        act = jax.nn.silu(jnp.einsum('bsd,df->bsf', h, w_gate[l])) * jnp.einsum('bsd,df->bsf', h, w_up[l])
        x = x + jnp.einsum('bsf,fd->bsd', act, w_down[l])
    return rmsnorm(x, g_final)
```

```python
import functools
import math

import jax
import jax.numpy as jnp
import numpy as np
from jax import lax
from jax.experimental import pallas as pl
from jax.experimental.pallas import tpu as pltpu

F32 = jnp.float32
BF16 = jnp.bfloat16

HEAD_DIM = 64
LANES = 128
N_PAIRS = 4
WIDTH = N_PAIRS * LANES
KV_A = LANES
BLOCK = 128
ROPE_THETA = 150000.0
REL_BUCKETS = 32
REL_MAX_DISTANCE = 2048
NORM_EPS = 1e-5
NEG = -1e30
DILATIONS = (1, 4, 16)
N_BACK_A = 127
N_BACK_B = 128

TM_PROJ = 512
TM_OUT = 512
TQ_MAX = 512
VMEM_LIMIT = 56 * 1024 * 1024


def _rms(x, g):
    ms = jnp.mean(x * x, axis=-1, keepdims=True)
    return x * lax.rsqrt(ms + NORM_EPS) * g


def _inproj_kernel(x_ref, g_ref, w_ref, b_ref, cos_ref, sin_ref,
                   qa_ref, ka_ref, va_ref, qb_ref, kb_ref, vb_ref):
    x = x_ref[0]
    h = _rms(x, g_ref[...]).astype(BF16)
    proj = jnp.dot(h, w_ref[...], preferred_element_type=F32) + b_ref[...]

    cos = cos_ref[...]
    sin = sin_ref[...]
    lane = lax.broadcasted_iota(jnp.int32, cos.shape, 1)
    first_half = (lane % HEAD_DIM) < (HEAD_DIM // 2)

    def rope(t):
        partner = jnp.where(first_half,
                            pltpu.roll(t, LANES - HEAD_DIM // 2, 1),
                            pltpu.roll(t, HEAD_DIM // 2, 1))
        return t * cos + partner * sin

    scale = HEAD_DIM ** -0.5
    for p in range(N_PAIRS):
        t = proj[:, p * LANES:(p + 1) * LANES]
        qa_ref[0, :, p * LANES:(p + 1) * LANES] = (rope(t) * scale).astype(BF16)
    off = WIDTH
    ka_ref[0] = rope(proj[:, off:off + KV_A]).astype(BF16)
    off += KV_A
    va_ref[0] = proj[:, off:off + KV_A].astype(BF16)
    off += KV_A
    qb_ref[0] = (proj[:, off:off + WIDTH] * scale).astype(BF16)
    off += WIDTH
    kb_ref[0] = proj[:, off:off + WIDTH].astype(BF16)
    off += WIDTH
    vb_ref[0] = proj[:, off:off + WIDTH].astype(BF16)


def _inproj(x, g, w, b, cos, sin):
    B, S, D = x.shape
    E = w.shape[1]
    tm = TM_PROJ
    const = lambda bb, i: (0, 0)
    tok = lambda bb, i: (bb, i, 0)
    out_w = (WIDTH, KV_A, KV_A, WIDTH, WIDTH, WIDTH)
    return pl.pallas_call(
        _inproj_kernel,
        grid=(B, S // tm),
        in_specs=[
            pl.BlockSpec((1, tm, D), tok),
            pl.BlockSpec((1, D), const),
            pl.BlockSpec((D, E), const, pipeline_mode=pl.Buffered(1)),
            pl.BlockSpec((1, E), const),
            pl.BlockSpec((tm, LANES), lambda bb, i: (i, 0)),
            pl.BlockSpec((tm, LANES), lambda bb, i: (i, 0)),
        ],
        out_specs=[pl.BlockSpec((1, tm, wd), tok) for wd in out_w],
        out_shape=[jax.ShapeDtypeStruct((B, S, wd), BF16) for wd in out_w],
        compiler_params=pltpu.CompilerParams(
            dimension_semantics=("parallel", "parallel"),
            vmem_limit_bytes=VMEM_LIMIT),
        name="inproj",
    )(x, g, w, b, cos, sin)


def _attn_kernel(*refs, tq, shared_kv, per_head_bias, has_sinks, want_lse):
    it = iter(refs)
    q_ref, kc_ref, kp_ref, vc_ref, vp_ref, bias_ref = (next(it) for _ in range(6))
    sink_ref = next(it) if has_sinks else None
    o_ref = next(it)
    lse_ref = next(it) if want_lse else None
    ks_ref, vs_ref = next(it), next(it)

    ks_ref[0:BLOCK, :] = kp_ref[0]
    ks_ref[BLOCK:, :] = kc_ref[0]
    vs_ref[0:BLOCK, :] = vp_ref[0]
    vs_ref[BLOCK:, :] = vc_ref[0]

    step = pl.program_id(2)
    lane = lax.broadcasted_iota(jnp.int32, (BLOCK, LANES), 1)
    lo = lane < HEAD_DIM

    def one_block(t, carry):
        r0 = pl.multiple_of(t * BLOCK, BLOCK)
        variant = jnp.where(jnp.logical_and(step == 0, t == 0), 0, 1)
        for p in range(N_PAIRS):
            kcol = 0 if shared_kv else p * LANES
            q2 = q_ref[0, pl.ds(r0, BLOCK), p * LANES:(p + 1) * LANES]
            kw = ks_ref[pl.ds(r0, 2 * BLOCK), kcol:kcol + LANES]
            vw = vs_ref[pl.ds(r0, 2 * BLOCK), kcol:kcol + LANES]
            outs, lses = [], []
            for half in range(2):
                head = (p + N_PAIRS * half) if shared_kv else (2 * p + half)
                qm = jnp.where(lo if half == 0 else jnp.logical_not(lo), q2, jnp.zeros_like(q2))
                s = lax.dot_general(qm, kw, (((1,), (1,)), ((), ())),
                                    preferred_element_type=F32)
                s = s + bias_ref[variant, head if per_head_bias else 0]
                m = jnp.max(s, axis=-1, keepdims=True)
                if has_sinks:
                    sink = sink_ref[head]
                    m = jnp.maximum(m, sink)
                pr = jnp.exp(s - m)
                den = jnp.sum(pr, axis=-1, keepdims=True)
                if has_sinks:
                    den = den + jnp.exp(sink - m)
                o = jnp.dot(pr.astype(BF16), vw, preferred_element_type=F32)
                outs.append(o * (1.0 / den))
                if want_lse:
                    lses.append(m + jnp.log(den))
            o_ref[0, pl.ds(r0, BLOCK), p * LANES:(p + 1) * LANES] = (
                jnp.where(lo, outs[0], outs[1]).astype(o_ref.dtype))
            if want_lse:
                lse_ref[0, pl.ds(r0, BLOCK), p * LANES:(p + 1) * LANES] = (
                    jnp.where(lo, lses[0], lses[1]))
        return carry

    lax.fori_loop(0, tq // BLOCK, one_block, 0)


def _attention(q, k, v, bias, sinks, *, dil, want_lse, name):
    B, S, _ = q.shape
    wk = k.shape[-1]
    shared_kv = wk == KV_A
    L = S // dil
    tq = min(L, TQ_MAX)
    nblk = tq // BLOCK
    qv = q.reshape(B, L, dil * WIDTH)
    kv = k.reshape(B, L, dil * wk)
    vv = v.reshape(B, L, dil * wk)

    cur = lambda b, r, i: (b, i, r)
    prev = lambda b, r, i: (b, jnp.maximum(i * nblk - 1, 0), r)
    in_specs = [
        pl.BlockSpec((1, tq, WIDTH), cur),
        pl.BlockSpec((1, tq, wk), cur),
        pl.BlockSpec((1, BLOCK, wk), prev),
        pl.BlockSpec((1, tq, wk), cur),
        pl.BlockSpec((1, BLOCK, wk), prev),
        pl.BlockSpec(bias.shape, lambda b, r, i: (0, 0, 0, 0), pipeline_mode=pl.Buffered(1)),
    ]
    args = [qv, kv, kv, vv, vv, bias]
    if sinks is not None:
        in_specs.append(pl.BlockSpec(memory_space=pltpu.SMEM))
        args.append(sinks)
    out_specs = [pl.BlockSpec((1, tq, WIDTH), cur)]
    out_shape = [jax.ShapeDtypeStruct((B, L, dil * WIDTH), BF16)]
    if want_lse:
        out_specs.append(pl.BlockSpec((1, tq, WIDTH), cur))
        out_shape.append(jax.ShapeDtypeStruct((B, L, dil * WIDTH), F32))
    kern = functools.partial(
        _attn_kernel, tq=tq, shared_kv=shared_kv, per_head_bias=bias.shape[1] > 1,
        has_sinks=sinks is not None, want_lse=want_lse)
    outs = pl.pallas_call(
        kern,
        grid=(B, dil, L // tq),
        in_specs=in_specs,
        out_specs=out_specs,
        out_shape=out_shape,
        scratch_shapes=[pltpu.VMEM((tq + BLOCK, wk), BF16),
                        pltpu.VMEM((tq + BLOCK, wk), BF16)],
        compiler_params=pltpu.CompilerParams(
            dimension_semantics=("parallel", "parallel", "parallel"),
            vmem_limit_bytes=VMEM_LIMIT),
        name=name,
    )(*args)
    return [o.reshape(B, S, WIDTH) for o in outs]


def _out_ffn_kernel(x_ref, oa_ref, o1_ref, o2_ref, o3_ref, l1_ref, l2_ref, l3_ref,
                    ga_ref, gb_ref, woa_ref, wob_ref, gf_ref, wg_ref, wu_ref, wd_ref,
                    gl_ref, out_ref):
    l1, l2, l3 = l1_ref[0], l2_ref[0], l3_ref[0]
    m = jnp.maximum(jnp.maximum(l1, l2), l3)
    e1, e2, e3 = jnp.exp(l1 - m), jnp.exp(l2 - m), jnp.exp(l3 - m)
    ob = (e1 * o1_ref[0].astype(F32) + e2 * o2_ref[0].astype(F32)
          + e3 * o3_ref[0].astype(F32)) / (e1 + e2 + e3)
    na = _rms(oa_ref[0].astype(F32), ga_ref[...]).astype(BF16)
    nb = _rms(ob, gb_ref[...]).astype(BF16)
    x1 = (x_ref[0]
          + jnp.dot(na, woa_ref[...], preferred_element_type=F32)
          + jnp.dot(nb, wob_ref[...], preferred_element_type=F32))
    h = _rms(x1, gf_ref[...]).astype(BF16)
    gate = jnp.dot(h, wg_ref[...], preferred_element_type=F32)
    up = jnp.dot(h, wu_ref[...], preferred_element_type=F32)
    act = (gate * jax.nn.sigmoid(gate) * up).astype(BF16)
    x2 = x1 + jnp.dot(act, wd_ref[...], preferred_element_type=F32)
    out_ref[0] = _rms(x2, gl_ref[...])


def _out_ffn(x, oa, obs, lses, ga, gb, woa, wob, gf, wg, wu, wd, gl):
    B, S, D = x.shape
    F = wg.shape[1]
    tm = TM_OUT
    tok = lambda bb, i: (bb, i, 0)
    const = lambda bb, i: (0, 0)
    res = lambda shape: pl.BlockSpec(shape, const, pipeline_mode=pl.Buffered(1))
    in_specs = ([pl.BlockSpec((1, tm, D), tok)]
                + [pl.BlockSpec((1, tm, WIDTH), tok)] * 7
                + [res((1, WIDTH)), res((1, WIDTH)), res((WIDTH, D)), res((WIDTH, D)),
                   res((1, D)), res((D, F)), res((D, F)), res((F, D)), res((1, D))])
    return pl.pallas_call(
        _out_ffn_kernel,
        grid=(B, S // tm),
        in_specs=in_specs,
        out_specs=pl.BlockSpec((1, tm, D), tok),
        out_shape=jax.ShapeDtypeStruct((B, S, D), F32),
        compiler_params=pltpu.CompilerParams(
            dimension_semantics=("parallel", "parallel"),
            vmem_limit_bytes=VMEM_LIMIT),
        name="out_ffn",
    )(x, oa, *obs, *lses, ga, gb, woa, wob, gf, wg, wu, wd, gl)


def _t5_bucket(dist):
    max_exact = REL_BUCKETS // 2
    df = jnp.maximum(dist, 1).astype(F32)
    large = max_exact + (jnp.log(df / max_exact) / math.log(REL_MAX_DISTANCE / max_exact)
                         * (REL_BUCKETS - max_exact)).astype(jnp.int32)
    large = jnp.minimum(large, REL_BUCKETS - 1)
    return jnp.where(dist < max_exact, dist, large)


def _band_bias(per_dist, n_back):
    delta = BLOCK + np.arange(BLOCK)[:, None] - np.arange(2 * BLOCK)[None, :]
    in_band = (delta >= 0) & (delta <= n_back)
    full = per_dist[:, np.clip(delta, 0, n_back)]
    normal = jnp.where(in_band[None], full, NEG)
    has_key = np.arange(2 * BLOCK)[None, :] >= BLOCK
    first = jnp.where((in_band & has_key)[None], full, NEG)
    return jnp.stack([first, normal], axis=0).astype(F32)


def _pair_perm():
    cols = []
    for p in range(N_PAIRS):
        cols.append(np.arange(p * HEAD_DIM, (p + 1) * HEAD_DIM))
        cols.append(np.arange((p + N_PAIRS) * HEAD_DIM, (p + N_PAIRS + 1) * HEAD_DIM))
    return np.concatenate(cols)


def kernel(x, g_attn, w_in, b_in, sinks, rel_table, g_out_a, g_out_b, w_o, g_ffn,
           w_gate, w_up, w_down, g_final):
    B, S, D = x.shape
    perm = _pair_perm()
    e_total = w_in.shape[-1]
    col_perm = np.concatenate([perm, np.arange(WIDTH, e_total)])

    half = HEAD_DIM // 2
    inv_freq = ROPE_THETA ** (-jnp.arange(half, dtype=F32) / half)
    ang = jnp.arange(S, dtype=F32)[:, None] * inv_freq[None, :]
    cos, sin = jnp.cos(ang), jnp.sin(ang)
    cos_t = jnp.concatenate([cos, cos, cos, cos], axis=-1)
    sin_t = jnp.concatenate([-sin, sin, -sin, sin], axis=-1)

    bias_a = _band_bias(jnp.zeros((1, N_BACK_A + 1), F32), N_BACK_A)
    bias_b = [
        _band_bias(rel_table[_t5_bucket(jnp.arange(N_BACK_B + 1) * dil)].T.astype(F32), N_BACK_B)
        for dil in DILATIONS
    ]

    for l in range(g_attn.shape[0]):
        w = w_in[l][:, col_perm].astype(BF16)
        b = b_in[l][col_perm][None, :]
        qa, ka, va, qb, kb, vb = _inproj(x, g_attn[l][None, :], w, b, cos_t, sin_t)

        (oa,) = _attention(qa, ka, va, bias_a, sinks[l], dil=1, want_lse=False, name="attn_a")
        obs, lses = [], []
        for dil, bias in zip(DILATIONS, bias_b):
            o, lse = _attention(qb, kb, vb, bias, None, dil=dil, want_lse=True,
                                name=f"attn_b{dil}")
            obs.append(o)
            lses.append(lse)

        wo = w_o[l]
        x = _out_ffn(
            x, oa, obs, lses,
            g_out_a[l][perm][None, :], g_out_b[l][None, :],
            wo[:WIDTH][perm].astype(BF16), wo[WIDTH:].astype(BF16),
            g_ffn[l][None, :], w_gate[l].astype(BF16), w_up[l].astype(BF16),
            w_down[l].astype(BF16), g_final[None, :])
    return x
```
